```python
import jax, jax.numpy as jnp
from jax import lax
import numpy as np

D_MODEL = 4096
BATCH = 4
SEQ = 4096
DEPTH = 1

D_MIX = D_MODEL
LRU_WIDTH = D_MIX // 2
LRU_HEADS = 16
LRU_BLOCK = LRU_WIDTH // LRU_HEADS
CONV_WIDTH = 4
LRU_C = 8.0
HG_WIDTH = D_MIX - LRU_WIDTH
HG_HEAD_DIM = 128
HG_HEADS = HG_WIDTH // HG_HEAD_DIM
CHUNK = 64
IN_COLS = 2 * LRU_WIDTH + 4 * HG_WIDTH
SPLITS = [LRU_WIDTH, 2 * LRU_WIDTH, 2 * LRU_WIDTH + HG_WIDTH, 2 * LRU_WIDTH + 2 * HG_WIDTH, 2 * LRU_WIDTH + 3 * HG_WIDTH]
D_FF = 3 * D_MODEL
N_MEM = 256
X_HEADS = 4
X_HEAD_DIM = D_MODEL // X_HEADS
EPS = 1e-6

kernel_name = "hymba_style_rglru_hgrn2_macaron_memxattn"


def rmsnorm(x, w):
    xf = x.astype(jnp.float32)
    y = xf * lax.rsqrt(jnp.mean(xf * xf, axis=-1, keepdims=True) + EPS)
    return (y * w.astype(jnp.float32)).astype(x.dtype)


def swiglu(x, w_gate, w_up, w_down):
    return (jax.nn.silu(x @ w_gate) * (x @ w_up)) @ w_down


def causal_depthwise_conv(x, w, b):
    y = lax.conv_general_dilated(x, w, window_strides=(1,), padding=[(CONV_WIDTH - 1, 0)],
                                 dimension_numbers=('NWC', 'WIO', 'NWC'), feature_group_count=x.shape[-1])
    return y + b


def rg_lru_group(xb, gate, conv_w, conv_b, w_a, b_a, w_x, b_x, lam):
    B, S, W = xb.shape
    xc = causal_depthwise_conv(xb, conv_w, conv_b)
    xh = xc.reshape(B, S, LRU_HEADS, LRU_BLOCK)
    r = jax.nn.sigmoid(jnp.einsum('bshi,hij->bshj', xh, w_a) + b_a).reshape(B, S, W)
    i = jax.nn.sigmoid(jnp.einsum('bshi,hij->bshj', xh, w_x) + b_x).reshape(B, S, W)
    log_a = -LRU_C * r.astype(jnp.float32) * jax.nn.softplus(-lam.astype(jnp.float32))
    a = jnp.exp(log_a)
    bval = jnp.sqrt(-jnp.expm1(2.0 * log_a)) * (i.astype(jnp.float32) * xc.astype(jnp.float32))

    def combine(l, rgt):
        a1, b1 = l
        a2, b2 = rgt
        return a1 * a2, a2 * b1 + b2

    _, h = lax.associative_scan(combine, (a, bval), axis=1)
    return (h * jax.nn.gelu(gate.astype(jnp.float32))).astype(xb.dtype)


def hgrn2_group(q, f_pre, v, g, lb, gnorm_w):
    B, S, _ = q.shape
    N = S // CHUNK
    f32 = jnp.float32
    lb = lb.astype(f32)
    f = lb + (1.0 - lb) * jax.nn.sigmoid(f_pre.astype(f32))
    log_f = jnp.log(f)
    k = 1.0 - f
    q = jax.nn.silu(q.astype(f32))
    v = v.astype(f32)

    def to_chunks(t):
        return t.reshape(B, N, CHUNK, HG_HEADS, HG_HEAD_DIM).transpose(1, 0, 3, 2, 4)

    causal = jnp.tril(jnp.ones((CHUNK, CHUNK), dtype=bool))[:, :, None]

    def chunk_step(state, inp):
        qc, kc, vc, lc = inp
        bcum = jnp.cumsum(lc, axis=2)
        o_inter = jnp.einsum('bhtk,bhkv->bhtv', qc * jnp.exp(bcum), state)
        diff = bcum[:, :, :, None, :] - bcum[:, :, None, :, :]
        decay = jnp.exp(jnp.where(causal, diff, -jnp.inf))
        scores = jnp.einsum('bhtk,bhsk,bhtsk->bhts', qc, kc, decay)
        o = o_inter + jnp.einsum('bhts,bhsv->bhtv', scores, vc)
        b_last = bcum[:, :, -1:, :]
        new_state = jnp.exp(b_last[:, :, 0, :])[..., None] * state + \
            jnp.einsum('bhsk,bhsv->bhkv', kc * jnp.exp(b_last - bcum), vc)
        return new_state, o

    s0 = jnp.zeros((B, HG_HEADS, HG_HEAD_DIM, HG_HEAD_DIM), f32)
    _, o = lax.scan(chunk_step, s0, (to_chunks(q), to_chunks(k), to_chunks(v), to_chunks(log_f)))
    o = o.transpose(1, 0, 3, 2, 4).reshape(B, S, HG_HEADS, HG_HEAD_DIM)
    o = o * lax.rsqrt(jnp.mean(o * o, axis=-1, keepdims=True) + EPS) * gnorm_w.astype(f32)
    o = o * jax.nn.silu(g.astype(f32)).reshape(B, S, HG_HEADS, HG_HEAD_DIM)
    return o.reshape(B, S, HG_WIDTH).astype(g.dtype)


def mem_cross_attention(xn, memn, w_q, w_k, w_v, w_o):
    B, S, _ = xn.shape
    q = (xn @ w_q).reshape(B, S, X_HEADS, X_HEAD_DIM)
    k = (memn @ w_k).reshape(B, N_MEM, X_HEADS, X_HEAD_DIM)
    v = (memn @ w_v).reshape(B, N_MEM, X_HEADS, X_HEAD_DIM)
    s = jnp.einsum('bshd,bmhd->bhsm', q.astype(jnp.float32), k.astype(jnp.float32)) * (X_HEAD_DIM ** -0.5)
    p = jax.nn.softmax(s, axis=-1).astype(v.dtype)
    o = jnp.einsum('bhsm,bmhd->bshd', p, v).reshape(B, S, D_MODEL)
    return o @ w_o


def setup_inputs(seed: int = 0) -> dict:
    key = jax.random.key(seed)
    ks = iter(jax.random.split(key, 40))
    f32 = jnp.float32

    def nrm(shape, fan_in):
        return jax.random.normal(next(ks), shape, f32) * (fan_in ** -0.5)

    def gain(shape):
        return 1.0 + 0.01 * jax.random.normal(next(ks), shape, f32)

    def bias(shape):
        return 0.01 * jax.random.normal(next(ks), shape, f32)

    L = DEPTH
    u = jax.random.uniform(next(ks), (L, LRU_WIDTH), f32, 0.9, 0.999)
    a0 = u ** (1.0 / LRU_C)
    lru_lambda = jnp.log(a0) - jnp.log1p(-a0)
    return {
        "x": jax.random.normal(next(ks), (BATCH, SEQ, D_MODEL), f32),
        "mem": jax.random.normal(next(ks), (BATCH, N_MEM, D_MODEL), f32),
        "ffn1_norm": gain((L, D_MODEL)),
        "ffn1_w_gate": nrm((L, D_MODEL, D_FF), D_MODEL),
        "ffn1_w_up": nrm((L, D_MODEL, D_FF), D_MODEL),
        "ffn1_w_down": nrm((L, D_FF, D_MODEL), D_FF),
        "mix_norm": gain((L, D_MODEL)),
        "w_in": nrm((L, D_MODEL, IN_COLS), D_MODEL),
        "conv_w": nrm((L, CONV_WIDTH, 1, LRU_WIDTH), CONV_WIDTH),
        "conv_b": bias((L, LRU_WIDTH)),
        "lru_w_a": nrm((L, LRU_HEADS, LRU_BLOCK, LRU_BLOCK), LRU_BLOCK),
        "lru_b_a": bias((L, LRU_HEADS, LRU_BLOCK)),
        "lru_w_x": nrm((L, LRU_HEADS, LRU_BLOCK, LRU_BLOCK), LRU_BLOCK),
        "lru_b_x": bias((L, LRU_HEADS, LRU_BLOCK)),
        "lru_lambda": lru_lambda,
        "hg_lb_logits": 0.5 * jax.random.normal(next(ks), (L + 1, HG_WIDTH), f32),
        "hg_gnorm": gain((L, HG_HEAD_DIM)),
        "w_out": nrm((L, D_MIX, D_MODEL), D_MIX),
        "xattn_norm": gain((L, D_MODEL)),
        "mem_norm": gain((L, D_MODEL)),
        "xattn_w_q": nrm((L, D_MODEL, D_MODEL), D_MODEL),
        "xattn_w_k": nrm((L, D_MODEL, D_MODEL), D_MODEL),
        "xattn_w_v": nrm((L, D_MODEL, D_MODEL), D_MODEL),
        "xattn_w_o": nrm((L, D_MODEL, D_MODEL), D_MODEL),
        "ffn2_norm": gain((L, D_MODEL)),
        "ffn2_w_gate": nrm((L, D_MODEL, D_FF), D_MODEL),
        "ffn2_w_up": nrm((L, D_MODEL, D_FF), D_MODEL),
        "ffn2_w_down": nrm((L, D_FF, D_MODEL), D_FF),
        "final_norm": gain((D_MODEL,)),
    }


def reference(x, mem, ffn1_norm, ffn1_w_gate, ffn1_w_up, ffn1_w_down, mix_norm, w_in, conv_w, conv_b,
              lru_w_a, lru_b_a, lru_w_x, lru_b_x, lru_lambda, hg_lb_logits, hg_gnorm, w_out,
              xattn_norm, mem_norm, xattn_w_q, xattn_w_k, xattn_w_v, xattn_w_o,
              ffn2_norm, ffn2_w_gate, ffn2_w_up, ffn2_w_down, final_norm):
    lower_bounds = jnp.cumsum(jax.nn.softmax(hg_lb_logits.astype(jnp.float32), axis=0), axis=0)
    h = x
    for l in range(DEPTH):
        h = h + 0.5 * swiglu(rmsnorm(h, ffn1_norm[l]), ffn1_w_gate[l], ffn1_w_up[l], ffn1_w_down[l])
        u = rmsnorm(h, mix_norm[l]) @ w_in[l]
        lru_x, lru_g, hq, hf, hi, hg = jnp.split(u, SPLITS, axis=-1)
        y_a = rg_lru_group(lru_x, lru_g, conv_w[l], conv_b[l], lru_w_a[l], lru_b_a[l],
                           lru_w_x[l], lru_b_x[l], lru_lambda[l])
        y_b = hgrn2_group(hq, hf, hi, hg, lower_bounds[l], hg_gnorm[l])
        h = h + jnp.concatenate([y_a, y_b], axis=-1) @ w_out[l]
        h = h + mem_cross_attention(rmsnorm(h, xattn_norm[l]), rmsnorm(mem, mem_norm[l]),
                                    xattn_w_q[l], xattn_w_k[l], xattn_w_v[l], xattn_w_o[l])
        h = h + 0.5 * swiglu(rmsnorm(h, ffn2_norm[l]), ffn2_w_gate[l], ffn2_w_up[l], ffn2_w_down[l])
    return rmsnorm(h, final_norm)
```

```python
import functools

import jax
import jax.numpy as jnp
from jax import lax
from jax.experimental import pallas as pl
from jax.experimental.pallas import tpu as pltpu

F32 = jnp.float32
BF16 = jnp.bfloat16
EPS = 1e-6
LRU_C = 8.0
LANES = 128
SUBLANES = 8
V7X_VMEM_BYTES = 64 * 1024 * 1024
VMEM_LIMIT = V7X_VMEM_BYTES - 8 * 1024 * 1024


def _params(semantics):
    return pltpu.CompilerParams(dimension_semantics=semantics, vmem_limit_bytes=VMEM_LIMIT)


def _sigmoid(x):
    return 1.0 / (1.0 + jnp.exp(-x))


def _silu(x):
    return x * _sigmoid(x)


_NORM_ROWS = 32


def _rms_to_scratch(x_ref, nw_ref, xn_ref):
    tm = x_ref.shape[0]
    nw = nw_ref[...]

    def body(c, carry):
        r = pl.multiple_of(c * _NORM_ROWS, _NORM_ROWS)
        x = x_ref[pl.ds(r, _NORM_ROWS), :]
        ms = jnp.mean(x * x, axis=-1, keepdims=True)
        xn_ref[pl.ds(r, _NORM_ROWS), :] = (x * lax.rsqrt(ms + EPS) * nw).astype(xn_ref.dtype)
        return carry

    lax.fori_loop(0, tm // _NORM_ROWS, body, 0)


def _norm_mm_kernel(x_ref, nw_ref, w_ref, o_ref, xn_ref, *, scale):
    @pl.when(pl.program_id(1) == 0)
    def _():
        _rms_to_scratch(x_ref, nw_ref, xn_ref)

    acc = jnp.dot(xn_ref[...], w_ref[...], preferred_element_type=F32)
    if scale != 1.0:
        acc = acc * scale
    o_ref[...] = acc.astype(o_ref.dtype)


def norm_matmul(x, nw, w, *, out_dtype, tm, tn, scale=1.0):
    M, K = x.shape
    N = w.shape[1]
    assert M % tm == 0 and N % tn == 0 and tm % _NORM_ROWS == 0
    return pl.pallas_call(
        functools.partial(_norm_mm_kernel, scale=scale),
        grid=(M // tm, N // tn),
        in_specs=[
            pl.BlockSpec((tm, K), lambda i, j: (i, 0)),
            pl.BlockSpec((1, K), lambda i, j: (0, 0)),
            pl.BlockSpec((K, tn), lambda i, j: (0, j)),
        ],
        out_specs=pl.BlockSpec((tm, tn), lambda i, j: (i, j)),
        out_shape=jax.ShapeDtypeStruct((M, N), out_dtype),
        scratch_shapes=[pltpu.VMEM((tm, K), BF16)],
        compiler_params=_params(("parallel", "arbitrary")),
        name="norm_matmul",
    )(x, nw.reshape(1, K), w)


def _norm_swiglu_kernel(x_ref, nw_ref, wg_ref, wu_ref, o_ref, xn_ref):
    @pl.when(pl.program_id(1) == 0)
    def _():
        _rms_to_scratch(x_ref, nw_ref, xn_ref)

    xn = xn_ref[...]
    g = jnp.dot(xn, wg_ref[...], preferred_element_type=F32)
    u = jnp.dot(xn, wu_ref[...], preferred_element_type=F32)
    o_ref[...] = (_silu(g) * u).astype(o_ref.dtype)


def norm_swiglu(x, nw, wg, wu, *, tm, tn):
    M, K = x.shape
    N = wg.shape[1]
    assert M % tm == 0 and N % tn == 0 and tm % _NORM_ROWS == 0
    return pl.pallas_call(
        _norm_swiglu_kernel,
        grid=(M // tm, N // tn),
        in_specs=[
            pl.BlockSpec((tm, K), lambda i, j: (i, 0)),
            pl.BlockSpec((1, K), lambda i, j: (0, 0)),
            pl.BlockSpec((K, tn), lambda i, j: (0, j)),
            pl.BlockSpec((K, tn), lambda i, j: (0, j)),
        ],
        out_specs=pl.BlockSpec((tm, tn), lambda i, j: (i, j)),
        out_shape=jax.ShapeDtypeStruct((M, N), BF16),
        scratch_shapes=[pltpu.VMEM((tm, K), BF16)],
        compiler_params=_params(("parallel", "arbitrary")),
        name="norm_swiglu",
    )(x, nw.reshape(1, K), wg, wu)


def _resid_mm_kernel(a_ref, w_ref, r_ref, o_ref, acc_ref, *, alpha, nk):
    k = pl.program_id(2)

    @pl.when(k == 0)
    def _():
        acc_ref[...] = jnp.zeros_like(acc_ref)

    acc_ref[...] += jnp.dot(a_ref[...], w_ref[...], preferred_element_type=F32)

    @pl.when(k == nk - 1)
    def _():
        o_ref[...] = r_ref[...] + alpha * acc_ref[...]


def resid_matmul(a, w, res, *, alpha, tm, tn, tk):
    M, K = a.shape
    N = w.shape[1]
    assert M % tm == 0 and N % tn == 0 and K % tk == 0
    nk = K // tk
    return pl.pallas_call(
        functools.partial(_resid_mm_kernel, alpha=alpha, nk=nk),
        grid=(M // tm, N // tn, nk),
        in_specs=[
            pl.BlockSpec((tm, tk), lambda i, j, k: (i, k)),
            pl.BlockSpec((tk, tn), lambda i, j, k: (k, j)),
            pl.BlockSpec((tm, tn), lambda i, j, k: (i, j)),
        ],
        out_specs=pl.BlockSpec((tm, tn), lambda i, j, k: (i, j)),
        out_shape=jax.ShapeDtypeStruct((M, N), F32),
        scratch_shapes=[pltpu.VMEM((tm, tn), F32)],
        compiler_params=_params(("parallel", "parallel", "arbitrary")),
        name="resid_matmul",
    )(a, w, res)


def _rmsnorm_kernel(x_ref, nw_ref, o_ref):
    x = x_ref[...]
    ms = jnp.mean(x * x, axis=-1, keepdims=True)
    o_ref[...] = x * lax.rsqrt(ms + EPS) * nw_ref[...]


def rmsnorm(x, nw, *, tm):
    M, K = x.shape
    assert M % tm == 0
    return pl.pallas_call(
        _rmsnorm_kernel,
        grid=(M // tm,),
        in_specs=[pl.BlockSpec((tm, K), lambda i: (i, 0)), pl.BlockSpec((1, K), lambda i: (0, 0))],
        out_specs=pl.BlockSpec((tm, K), lambda i: (i, 0)),
        out_shape=jax.ShapeDtypeStruct((M, K), F32),
        compiler_params=_params(("parallel",)),
        name="final_rmsnorm",
    )(x, nw.reshape(1, K))


def _attn_kernel(q_ref, k_ref, v_ref, o_ref):
    s = lax.dot_general(q_ref[...], k_ref[...], (((1,), (1,)), ((), ())), preferred_element_type=F32)
    m = jnp.max(s, axis=-1, keepdims=True)
    e = jnp.exp(s - m)
    p = e / jnp.sum(e, axis=-1, keepdims=True)
    o_ref[...] = jnp.dot(p.astype(BF16), v_ref[...], preferred_element_type=F32).astype(o_ref.dtype)


def mem_attention(q, kv, *, batch, n_heads, tq):
    M, D = q.shape
    dh = D // n_heads
    S = M // batch
    n_mem = kv.shape[0] // batch
    assert S % tq == 0
    nq = S // tq
    return pl.pallas_call(
        _attn_kernel,
        grid=(batch, nq, n_heads),
        in_specs=[
            pl.BlockSpec((tq, dh), lambda b, i, h: (b * nq + i, h)),
            pl.BlockSpec((n_mem, dh), lambda b, i, h: (b, h)),
            pl.BlockSpec((n_mem, dh), lambda b, i, h: (b, n_heads + h)),
        ],
        out_specs=pl.BlockSpec((tq, dh), lambda b, i, h: (b * nq + i, h)),
        out_shape=jax.ShapeDtypeStruct((M, D), BF16),
        compiler_params=_params(("parallel", "parallel", "parallel")),
        name="mem_attention",
    )(q, kv, kv)


_LRU_ROWS = 64
_CONV_HIST = SUBLANES


def _lru_kernel(x_ref, g_ref, cw_ref, cb_ref, wa_ref, ba_ref, wx_ref, bx_ref, lam_ref,
                o_ref, ext_ref, a_ref, b_ref, h_ref, *, conv_width):
    T, CW = x_ref.shape
    n_heads_blk = wa_ref.shape[0]
    hd = CW // n_heads_blk
    n = pl.program_id(2)

    @pl.when(n == 0)
    def _():
        ext_ref[0:_CONV_HIST, :] = jnp.zeros((_CONV_HIST, CW), F32)
        h_ref[...] = jnp.zeros_like(h_ref)

    ext_ref[_CONV_HIST:_CONV_HIST + T, :] = x_ref[...]

    z = -lam_ref[...]
    softplus = jnp.maximum(z, 0.0) + jnp.log1p(jnp.exp(-jnp.abs(z)))
    cw = cw_ref[...]
    cb = cb_ref[...]
    ba = ba_ref[...]
    bx = bx_ref[...]

    def chunk(c, carry):
        r = pl.multiple_of(c * _LRU_ROWS, _LRU_ROWS)
        ext = ext_ref[pl.ds(r, _LRU_ROWS + _CONV_HIST), :]
        xc = cb + cw[conv_width - 1:conv_width, :] * ext[_CONV_HIST:, :]
        for shift in range(1, conv_width):
            j = conv_width - 1 - shift
            xc = xc + cw[j:j + 1, :] * pltpu.roll(ext, shift, axis=0)[_CONV_HIST:, :]
        for i in range(n_heads_blk):
            sl = slice(i * hd, (i + 1) * hd)
            xh = xc[:, sl]
            xb = xh.astype(BF16)
            rg = _sigmoid(jnp.dot(xb, wa_ref[i], preferred_element_type=F32) + ba[:, sl])
            ig = _sigmoid(jnp.dot(xb, wx_ref[i], preferred_element_type=F32) + bx[:, sl])
            log_a = (-LRU_C * rg) * softplus[:, sl]
            a_ref[pl.ds(r, _LRU_ROWS), sl] = jnp.exp(log_a)
            th = jnp.tanh(log_a)
            b_ref[pl.ds(r, _LRU_ROWS), sl] = jnp.sqrt(-2.0 * th / (1.0 - th)) * (ig * xh)
        return carry

    lax.fori_loop(0, T // _LRU_ROWS, chunk, 0)

    row = lax.broadcasted_iota(jnp.int32, (SUBLANES, CW), 0)

    def group(gi, hprev):
        r = pl.multiple_of(gi * SUBLANES, SUBLANES)
        a = a_ref[pl.ds(r, SUBLANES), :]
        b = b_ref[pl.ds(r, SUBLANES), :]
        for s in (1, 2, 4):
            keep = row >= s
            b = jnp.where(keep, a * pltpu.roll(b, s, axis=0) + b, b)
            a = jnp.where(keep, a * pltpu.roll(a, s, axis=0), a)
        h = a * hprev + b
        b_ref[pl.ds(r, SUBLANES), :] = h
        return h[SUBLANES - 1:SUBLANES, :]

    h_last = lax.fori_loop(0, T // SUBLANES, group, h_ref[...])
    h_ref[...] = h_last
    ext_ref[0:_CONV_HIST, :] = ext_ref[T:T + _CONV_HIST, :]

    def gate(c, carry):
        r = pl.multiple_of(c * _LRU_ROWS, _LRU_ROWS)
        gv = g_ref[pl.ds(r, _LRU_ROWS), :]
        o_ref[pl.ds(r, _LRU_ROWS), :] = (b_ref[pl.ds(r, _LRU_ROWS), :] * jax.nn.gelu(gv)).astype(o_ref.dtype)
        return carry

    lax.fori_loop(0, T // _LRU_ROWS, gate, 0)


def rg_lru(u, conv_w, conv_b, w_a, b_a, w_x, b_x, lam, *, batch, width, x_col, gate_col, t_blk, c_blk):
    M = u.shape[0]
    S = M // batch
    n_heads, hd, _ = w_a.shape
    conv_width = conv_w.shape[0]
    assert S % t_blk == 0 and width % c_blk == 0 and c_blk % hd == 0 and t_blk % _LRU_ROWS == 0
    assert x_col % c_blk == 0 and gate_col % c_blk == 0 and conv_width - 1 <= _CONV_HIST
    nt = S // t_blk
    nc = width // c_blk
    hpb = c_blk // hd
    xc0, gc0 = x_col // c_blk, gate_col // c_blk
    vec = lambda b, c, n: (0, c)
    return pl.pallas_call(
        functools.partial(_lru_kernel, conv_width=conv_width),
        grid=(batch, nc, nt),
        in_specs=[
            pl.BlockSpec((t_blk, c_blk), lambda b, c, n: (b * nt + n, xc0 + c)),
            pl.BlockSpec((t_blk, c_blk), lambda b, c, n: (b * nt + n, gc0 + c)),
            pl.BlockSpec((conv_width, c_blk), vec),
            pl.BlockSpec((1, c_blk), vec),
            pl.BlockSpec((hpb, hd, hd), lambda b, c, n: (c, 0, 0)),
            pl.BlockSpec((1, c_blk), vec),
            pl.BlockSpec((hpb, hd, hd), lambda b, c, n: (c, 0, 0)),
            pl.BlockSpec((1, c_blk), vec),
            pl.BlockSpec((1, c_blk), vec),
        ],
        out_specs=pl.BlockSpec((t_blk, c_blk), lambda b, c, n: (b * nt + n, c)),
        out_shape=jax.ShapeDtypeStruct((M, width), BF16),
        scratch_shapes=[
            pltpu.VMEM((t_blk + _CONV_HIST, c_blk), F32),
            pltpu.VMEM((t_blk, c_blk), F32),
            pltpu.VMEM((t_blk, c_blk), F32),
            pltpu.VMEM((1, c_blk), F32),
        ],
        compiler_params=_params(("parallel", "parallel", "arbitrary")),
        name="rg_lru",
    )(u, u, conv_w.reshape(conv_width, width), conv_b.reshape(1, width), w_a,
      b_a.reshape(1, width), w_x, b_x.reshape(1, width), lam.reshape(1, width))


_HG_BLK = 64
_HG_SUB = SUBLANES


def _hgrn_kernel(q_ref, f_ref, v_ref, g_ref, lbl_ref, gn_ref, o_ref, *, layer):
    S, D = q_ref.shape
    T, C = _HG_BLK, _HG_SUB
    nsub = T // C

    lg = lbl_ref[...]
    e = jnp.exp(lg - jnp.max(lg, axis=0, keepdims=True))
    lb = jnp.sum(e[0:layer + 1, :], axis=0, keepdims=True) / jnp.sum(e, axis=0, keepdims=True)
    gn = gn_ref[...]

    rr = lax.broadcasted_iota(jnp.int32, (T, T), 0)
    cc = lax.broadcasted_iota(jnp.int32, (T, T), 1)
    tri = (cc <= rr).astype(BF16)
    off_mask = cc < (rr // C) * C
    ones = jnp.ones((D, T), BF16)
    sub3 = lax.broadcasted_iota(jnp.int32, (nsub, C, D), 1)

    def block(n, st):
        r = pl.multiple_of(n * T, T)
        qp = q_ref[pl.ds(r, T), :]
        q = _silu(qp)
        f = lb + (1.0 - lb) * _sigmoid(f_ref[pl.ds(r, T), :])
        lf = jnp.log(f)
        k = 1.0 - f
        v = v_ref[pl.ds(r, T), :]
        vb = v.astype(BF16)

        p1 = lf.astype(BF16)
        r1 = lf - p1.astype(F32)
        p2 = r1.astype(BF16)
        p3 = (r1 - p2.astype(F32)).astype(BF16)
        bc = (jnp.dot(tri, p1, preferred_element_type=F32) + jnp.dot(tri, p2, preferred_element_type=F32)
              + jnp.dot(tri, p3, preferred_element_type=F32))
        b_last = bc[T - 1:T, :]

        qe = (q * jnp.exp(bc)).astype(BF16)
        o = lax.dot_general(qe, st.astype(BF16), (((1,), (1,)), ((), ())), preferred_element_type=F32)

        parts = [jnp.zeros((C, T), F32)]
        for i in range(1, nsub):
            ref_pt = bc[i * C - 1:i * C, :]
            qi = (q[i * C:(i + 1) * C, :] * jnp.exp(bc[i * C:(i + 1) * C, :] - ref_pt)).astype(BF16)
            ki = (k * jnp.exp(jnp.minimum(ref_pt - bc, 0.0))).astype(BF16)
            parts.append(lax.dot_general(qi, ki, (((1,), (1,)), ((), ())), preferred_element_type=F32))
        scores = jnp.where(off_mask, jnp.concatenate(parts, axis=0), 0.0)

        q3 = q.reshape(nsub, C, D)
        k3 = k.reshape(nsub, C, D)
        b3 = bc.reshape(nsub, C, D)
        xs = []
        for d in range(C):
            kd = pltpu.roll(k3, d, axis=1) if d else k3
            bd = pltpu.roll(b3, d, axis=1) if d else b3
            x = q3 * kd * jnp.exp(jnp.minimum(b3 - bd, 0.0))
            xs.append(x.reshape(T, D).astype(BF16))
        rowsum = jnp.dot(jnp.concatenate(xs, axis=0), ones, preferred_element_type=F32)
        for d in range(C):
            sel = (cc == rr - d) & ((rr % C) >= d)
            scores = scores + jnp.where(sel, rowsum[d * T:(d + 1) * T, :], 0.0)

        o = o + jnp.dot(scores.astype(BF16), vb, preferred_element_type=F32)

        kst = (k * jnp.exp(b_last - bc)).astype(BF16)
        st_new = st * jnp.exp(b_last) + jnp.dot(v.T.astype(BF16), kst, preferred_element_type=F32)

        o = o * lax.rsqrt(jnp.mean(o * o, axis=-1, keepdims=True) + EPS) * gn
        o = o * _silu(g_ref[pl.ds(r, T), :])
        o_ref[pl.ds(r, T), :] = o.astype(o_ref.dtype)
        return st_new

    lax.fori_loop(0, S // T, block, jnp.zeros((D, D), F32))


def hgrn2(u, lb_logits, gnorm_w, *, batch, n_heads, head_dim, q_col, f_col, v_col, g_col, layer):
    M = u.shape[0]
    S = M // batch
    D = head_dim
    assert D == LANES and S % _HG_BLK == 0
    for col in (q_col, f_col, v_col, g_col):
        assert col % D == 0
    nl = lb_logits.shape[0]

    def col_spec(col):
        return pl.BlockSpec((S, D), lambda b, h: (b, col // D + h))

    return pl.pallas_call(
        functools.partial(_hgrn_kernel, layer=layer),
        grid=(batch, n_heads),
        in_specs=[
            col_spec(q_col), col_spec(f_col), col_spec(v_col), col_spec(g_col),
            pl.BlockSpec((nl, D), lambda b, h: (0, h)),
            pl.BlockSpec((1, D), lambda b, h: (0, 0)),
        ],
        out_specs=pl.BlockSpec((S, D), lambda b, h: (b, h)),
        out_shape=jax.ShapeDtypeStruct((M, n_heads * D), BF16),
        compiler_params=_params(("parallel", "parallel")),
        name="hgrn2",
    )(u, u, u, u, lb_logits, gnorm_w.reshape(1, D))


def kernel(x, mem, ffn1_norm, ffn1_w_gate, ffn1_w_up, ffn1_w_down, mix_norm, w_in, conv_w, conv_b, lru_w_a, lru_b_a, lru_w_x, lru_b_x, lru_lambda, hg_lb_logits, hg_gnorm, w_out, xattn_norm, mem_norm, xattn_w_q, xattn_w_k, xattn_w_v, xattn_w_o, ffn2_norm, ffn2_w_gate, ffn2_w_up, ffn2_w_down, final_norm):
    B, S, D = x.shape
    n_mem = mem.shape[1]
    depth = ffn1_norm.shape[0]
    lru_heads, lru_blk = lru_w_a.shape[1], lru_w_a.shape[2]
    lru_width = lru_heads * lru_blk
    hg_dim = hg_gnorm.shape[1]
    hg_width = (w_in.shape[2] - 2 * lru_width) // 4
    hg_heads = hg_width // hg_dim
    x_heads = 4
    x_head_dim = D // x_heads
    M = B * S

    bf = lambda w: w.astype(BF16)
    h = x.reshape(M, D)
    memf = mem.reshape(B * n_mem, D)

    def ffn(h, nw, wg, wu, wd):
        act = norm_swiglu(h, nw, bf(wg), bf(wu), tm=512, tn=512)
        return resid_matmul(act, bf(wd), h, alpha=0.5, tm=1024, tn=1024, tk=2048)

    for l in range(depth):
        h = ffn(h, ffn1_norm[l], ffn1_w_gate[l], ffn1_w_up[l], ffn1_w_down[l])

        u = norm_matmul(h, mix_norm[l], bf(w_in[l]), out_dtype=F32, tm=512, tn=1024)
        y_a = rg_lru(u, conv_w[l], conv_b[l], bf(lru_w_a[l]), lru_b_a[l], bf(lru_w_x[l]), lru_b_x[l],
                     lru_lambda[l], batch=B, width=lru_width, x_col=0, gate_col=lru_width,
                     t_blk=512, c_blk=512)
        y_b = hgrn2(u, hg_lb_logits, hg_gnorm[l], batch=B, n_heads=hg_heads, head_dim=hg_dim,
                    q_col=2 * lru_width, f_col=2 * lru_width + hg_width,
                    v_col=2 * lru_width + 2 * hg_width, g_col=2 * lru_width + 3 * hg_width, layer=l)
        y = jnp.concatenate([y_a, y_b], axis=-1)
        h = resid_matmul(y, bf(w_out[l]), h, alpha=1.0, tm=1024, tn=1024, tk=2048)

        q = norm_matmul(h, xattn_norm[l], bf(xattn_w_q[l]), out_dtype=BF16, tm=512, tn=1024,
                        scale=x_head_dim ** -0.5)
        w_kv = jnp.concatenate([bf(xattn_w_k[l]), bf(xattn_w_v[l])], axis=1)
        kv = norm_matmul(memf, mem_norm[l], w_kv, out_dtype=BF16, tm=512, tn=1024)
        o = mem_attention(q, kv, batch=B, n_heads=x_heads, tq=1024)
        h = resid_matmul(o, bf(xattn_w_o[l]), h, alpha=1.0, tm=1024, tn=1024, tk=2048)

        h = ffn(h, ffn2_norm[l], ffn2_w_gate[l], ffn2_w_up[l], ffn2_w_down[l])

    return rmsnorm(h, final_norm, tm=256).reshape(B, S, D)
```
